```python
import math
import jax, jax.numpy as jnp
from jax import lax
import numpy as np

D_MODEL = 2048
BATCH = 2
SEQ = 4096
DEPTH = 4
DEC_BATCH = 8
DEC_SEQ = 1
PAST_LEN = 16384
PAGE_SIZE = 128

HEAD_DIM = 128
FOX_HEADS = 8
FOX_WIDTH = FOX_HEADS * HEAD_DIM
HG_HEADS = 8
HG_DK = 128
HG_DV = 128
HG_KEY_WIDTH = HG_HEADS * HG_DK
HG_VAL_WIDTH = HG_HEADS * HG_DV
HG_CHUNK = 32
Q_BLOCK = 128
D_LRU = D_MODEL
LRU_BLOCKS = 16
LRU_BW = D_LRU // LRU_BLOCKS
LRU_C = 8.0
CONV_W = 4
D_FF = 4 * D_MODEL
N_EVEN = (DEPTH + 1) // 2
N_ODD = DEPTH // 2
EVEN_IN = 3 * FOX_WIDTH + FOX_HEADS + 2 * HG_KEY_WIDTH + 2 * HG_VAL_WIDTH
EVEN_CAT = FOX_WIDTH + HG_VAL_WIDTH
RMS_EPS = 1e-6
F_BIAS_CENTER = 3.0

kernel_name = "fox_hgrn2_rglru_hybrid_step"


def rmsnorm(x, g):
    xf = x.astype(jnp.float32)
    y = xf * lax.rsqrt(jnp.mean(xf * xf, axis=-1, keepdims=True) + RMS_EPS)
    return (y * g.astype(jnp.float32)).astype(x.dtype)


def sq_relu_mlp(h, w1, w2):
    return jnp.square(jax.nn.relu(h @ w1)) @ w2


def fox_attend(q, k, v, cq, ck, q_pos, k_pos):
    s = jnp.einsum('bqhd,bshd->bhqs', q, k).astype(jnp.float32) * (HEAD_DIM ** -0.5)
    bias = jnp.transpose(cq, (0, 2, 1))[:, :, :, None] - jnp.transpose(ck, (0, 2, 1))[:, :, None, :]
    mask = k_pos[None, :] <= q_pos[:, None]
    s = jnp.where(mask[None, None], s + bias, -jnp.inf)
    p = jax.nn.softmax(s, axis=-1).astype(v.dtype)
    return jnp.einsum('bhqs,bshd->bqhd', p, v)


def fox_prompt(q, k, v, logf):
    B, T = q.shape[0], q.shape[1]
    cum = jnp.cumsum(logf, axis=1)
    nb = T // Q_BLOCK
    qb = q.reshape(B, nb, Q_BLOCK, FOX_HEADS, HEAD_DIM).transpose(1, 0, 2, 3, 4)
    cb = cum.reshape(B, nb, Q_BLOCK, FOX_HEADS).transpose(1, 0, 2, 3)
    pos = jnp.arange(T, dtype=jnp.int32)
    pb = pos.reshape(nb, Q_BLOCK)

    def blk(args):
        qi, ci, pi = args
        return fox_attend(qi, k, v, ci, cum, pi, pos)

    o = lax.map(blk, (qb, cb, pb))
    return o.transpose(1, 0, 2, 3, 4).reshape(B, T, FOX_HEADS, HEAD_DIM)


def make_fox_sample(cache_k, cache_v, cache_logf, page_table, li):
    def fn(q, k, v, logf):
        DB, T = q.shape[0], q.shape[1]
        past = page_table.shape[1] * cache_k.shape[2]
        kp = cache_k[page_table, li].reshape(DB, past, FOX_HEADS, HEAD_DIM).astype(k.dtype)
        vp = cache_v[page_table, li].reshape(DB, past, FOX_HEADS, HEAD_DIM).astype(v.dtype)
        fp = cache_logf[page_table, li].reshape(DB, past, FOX_HEADS).astype(jnp.float32)
        k_all = jnp.concatenate([kp, k], axis=1)
        v_all = jnp.concatenate([vp, v], axis=1)
        cum = jnp.cumsum(jnp.concatenate([fp, logf], axis=1), axis=1)
        q_pos = past + jnp.arange(T, dtype=jnp.int32)
        k_pos = jnp.arange(past + T, dtype=jnp.int32)
        return fox_attend(q, k_all, v_all, cum[:, past:], cum, q_pos, k_pos)
    return fn


def hgrn2_chunked(q, k, v, logf, S0):
    B, T = q.shape[0], q.shape[1]
    nc = -(-T // HG_CHUNK)
    pad = nc * HG_CHUNK - T

    def prep(a):
        a = jnp.pad(a.astype(jnp.float32), ((0, 0), (0, pad), (0, 0), (0, 0)))
        return a.reshape(B, nc, HG_CHUNK, a.shape[2], a.shape[3]).transpose(1, 0, 3, 2, 4)

    tril = jnp.tril(jnp.ones((HG_CHUNK, HG_CHUNK), dtype=bool))

    def step(S, inp):
        qc, kc, vc, gc = inp
        bc = jnp.cumsum(gc, axis=2)
        diff = bc[:, :, :, None, :] - bc[:, :, None, :, :]
        decay = jnp.exp(jnp.where(tril[None, None, :, :, None], diff, -jnp.inf))
        A = jnp.einsum('bhtd,bhsd,bhtsd->bhts', qc, kc, decay)
        o = jnp.einsum('bhts,bhsv->bhtv', A, vc) + jnp.einsum('bhtd,bhdv->bhtv', qc * jnp.exp(bc), S)
        blast = bc[:, :, -1:, :]
        S_new = jnp.exp(blast[:, :, 0, :])[..., None] * S + jnp.einsum('bhsd,bhsv->bhdv', kc * jnp.exp(blast - bc), vc)
        return S_new, o

    S_fin, o = lax.scan(step, S0.astype(jnp.float32), (prep(q), prep(k), prep(v), prep(logf)))
    o = o.transpose(1, 0, 3, 2, 4).reshape(B, nc * HG_CHUNK, HG_HEADS, HG_DV)[:, :T]
    return o, S_fin


def even_mixer(h, w_in, b_f, lb, g_norm, w_out, fox_fn, S0):
    B, T, _ = h.shape
    z = h @ w_in
    sizes = (FOX_WIDTH, FOX_WIDTH, FOX_WIDTH, FOX_HEADS, HG_KEY_WIDTH, HG_KEY_WIDTH, HG_VAL_WIDTH, HG_VAL_WIDTH)
    offs = np.cumsum(sizes)[:-1].tolist()
    q, k, v, fl, hq, hf, hi, hg = jnp.split(z, offs, axis=-1)
    q = q.reshape(B, T, FOX_HEADS, HEAD_DIM)
    k = k.reshape(B, T, FOX_HEADS, HEAD_DIM)
    v = v.reshape(B, T, FOX_HEADS, HEAD_DIM)
    fox_logf = jax.nn.log_sigmoid((fl + b_f).astype(jnp.float32))
    o_fox = fox_fn(q, k, v, fox_logf)
    hg_logf = jnp.logaddexp(jnp.log(lb), jnp.log1p(-lb) + jax.nn.log_sigmoid(hf.astype(jnp.float32)))
    hg_k = -jnp.expm1(hg_logf)
    o_hg, S_new = hgrn2_chunked(jax.nn.silu(hq).reshape(B, T, HG_HEADS, HG_DK),
                                hg_k.reshape(B, T, HG_HEADS, HG_DK),
                                hi.reshape(B, T, HG_HEADS, HG_DV),
                                hg_logf.reshape(B, T, HG_HEADS, HG_DK), S0)
    o_hg = rmsnorm(o_hg, g_norm.reshape(HG_HEADS, HG_DV)).astype(h.dtype) * jax.nn.silu(hg).reshape(B, T, HG_HEADS, HG_DV)
    y = jnp.concatenate([o_fox.reshape(B, T, FOX_WIDTH), o_hg.reshape(B, T, HG_VAL_WIDTH)], axis=-1) @ w_out
    return y, k, v, fox_logf, S_new.astype(h.dtype)


def lin_combine(left, right):
    a1, b1 = left
    a2, b2 = right
    return a1 * a2, a2 * b1 + b2


def odd_mixer(h, w_in, conv_w, conv_b, wa, ba, wx, bx, lam, w_out, conv_state, h0):
    B, T, _ = h.shape
    z = h @ w_in
    gate, u = jnp.split(z, 2, axis=-1)
    xp = jnp.concatenate([conv_state.astype(u.dtype), u], axis=1)
    c = conv_b + sum(xp[:, j:j + T] * conv_w[j] for j in range(CONV_W))
    new_conv = xp[:, T:]
    xb = c.reshape(B, T, LRU_BLOCKS, LRU_BW)
    r = jax.nn.sigmoid((jnp.einsum('btnd,nde->btne', xb, wa).reshape(B, T, D_LRU) + ba).astype(jnp.float32))
    ig = jax.nn.sigmoid((jnp.einsum('btnd,nde->btne', xb, wx).reshape(B, T, D_LRU) + bx).astype(jnp.float32))
    log_a = -LRU_C * r * jax.nn.softplus(-lam.astype(jnp.float32))
    a = jnp.exp(log_a)
    b = jnp.sqrt(-jnp.expm1(2.0 * log_a)) * (ig * c.astype(jnp.float32))
    A, Bc = lax.associative_scan(lin_combine, (a, b), axis=1)
    hs = A * h0.astype(jnp.float32)[:, None, :] + Bc
    y = (jax.nn.gelu(gate, approximate=True) * hs.astype(h.dtype)) @ w_out
    return y, new_conv, hs[:, -1].astype(h.dtype)


def setup_inputs(seed: int = 0) -> dict:
    key = jax.random.key(seed)
    ks = jax.random.split(key, 32)
    f32 = jnp.float32
    n_pages = PAST_LEN // PAGE_SIZE
    n_phys = (DEC_BATCH * n_pages * 5) // 4
    nrm = lambda k, s, sc: jax.random.normal(k, s, f32) * sc
    x_prompt = nrm(ks[0], (BATCH, SEQ, D_MODEL), 1.0)
    x_sample = nrm(ks[1], (DEC_BATCH, DEC_SEQ, D_MODEL), 1.0)
    cache_k = nrm(ks[2], (n_phys, N_EVEN, PAGE_SIZE, FOX_HEADS, HEAD_DIM), 1.0)
    cache_v = nrm(ks[3], (n_phys, N_EVEN, PAGE_SIZE, FOX_HEADS, HEAD_DIM), 1.0)
    cache_logf = jax.nn.log_sigmoid(F_BIAS_CENTER + jax.random.normal(ks[4], (n_phys, N_EVEN, PAGE_SIZE, FOX_HEADS), f32))
    state_hgrn = nrm(ks[5], (DEC_BATCH, N_EVEN, HG_HEADS, HG_DK, HG_DV), 0.5)
    state_lru = nrm(ks[6], (DEC_BATCH, N_ODD, D_LRU), 0.5)
    state_conv = nrm(ks[7], (DEC_BATCH, N_ODD, CONV_W - 1, D_LRU), 1.0)
    page_table = jax.random.permutation(ks[8], n_phys)[:DEC_BATCH * n_pages].reshape(DEC_BATCH, n_pages).astype(jnp.int32)
    norm_mix = 1.0 + nrm(ks[9], (DEPTH, D_MODEL), 0.02)
    norm_mlp = 1.0 + nrm(ks[10], (DEPTH, D_MODEL), 0.02)
    norm_final = 1.0 + nrm(ks[11], (D_MODEL,), 0.02)
    w_in_even = nrm(ks[12], (N_EVEN, D_MODEL, EVEN_IN), D_MODEL ** -0.5)
    b_fox_f = F_BIAS_CENTER + nrm(ks[13], (N_EVEN, FOX_HEADS), 0.5)
    hg_lb_logits = nrm(ks[14], (N_EVEN, HG_KEY_WIDTH), 0.5)
    hg_norm = 1.0 + nrm(ks[15], (N_EVEN, HG_VAL_WIDTH), 0.02)
    w_out_even = nrm(ks[16], (N_EVEN, EVEN_CAT, D_MODEL), EVEN_CAT ** -0.5)
    w_in_odd = nrm(ks[17], (N_ODD, D_MODEL, 2 * D_LRU), D_MODEL ** -0.5)
    conv_w = nrm(ks[18], (N_ODD, CONV_W, D_LRU), CONV_W ** -0.5)
    conv_b = nrm(ks[19], (N_ODD, D_LRU), 0.01)
    lru_wa = nrm(ks[20], (N_ODD, LRU_BLOCKS, LRU_BW, LRU_BW), LRU_BW ** -0.5)
    lru_ba = nrm(ks[21], (N_ODD, D_LRU), 0.01)
    lru_wx = nrm(ks[22], (N_ODD, LRU_BLOCKS, LRU_BW, LRU_BW), LRU_BW ** -0.5)
    lru_bx = nrm(ks[23], (N_ODD, D_LRU), 0.01)
    a_c = jax.random.uniform(ks[24], (N_ODD, D_LRU), f32, 0.9, 0.999)
    a0 = a_c ** (1.0 / LRU_C)
    lru_lambda = jnp.log(a0) - jnp.log1p(-a0)
    w_out_odd = nrm(ks[25], (N_ODD, D_LRU, D_MODEL), D_LRU ** -0.5)
    mlp_w1 = nrm(ks[26], (DEPTH, D_MODEL, D_FF), D_MODEL ** -0.5)
    mlp_w2 = nrm(ks[27], (DEPTH, D_FF, D_MODEL), 0.5 * D_FF ** -0.5)
    return {"x_prompt": x_prompt, "x_sample": x_sample, "cache_k": cache_k, "cache_v": cache_v,
            "cache_logf": cache_logf, "state_hgrn": state_hgrn, "state_lru": state_lru,
            "state_conv": state_conv, "page_table": page_table, "norm_mix": norm_mix,
            "norm_mlp": norm_mlp, "norm_final": norm_final, "w_in_even": w_in_even,
            "b_fox_f": b_fox_f, "hg_lb_logits": hg_lb_logits, "hg_norm": hg_norm,
            "w_out_even": w_out_even, "w_in_odd": w_in_odd, "conv_w": conv_w, "conv_b": conv_b,
            "lru_wa": lru_wa, "lru_ba": lru_ba, "lru_wx": lru_wx, "lru_bx": lru_bx,
            "lru_lambda": lru_lambda, "w_out_odd": w_out_odd, "mlp_w1": mlp_w1, "mlp_w2": mlp_w2}


def reference(x_prompt, x_sample, cache_k, cache_v, cache_logf, state_hgrn, state_lru, state_conv,
              page_table, norm_mix, norm_mlp, norm_final, w_in_even, b_fox_f, hg_lb_logits, hg_norm,
              w_out_even, w_in_odd, conv_w, conv_b, lru_wa, lru_ba, lru_wx, lru_bx, lru_lambda,
              w_out_odd, mlp_w1, mlp_w2):
    Bp = x_prompt.shape[0]
    lbc = jnp.cumsum(jax.nn.softmax(hg_lb_logits.astype(jnp.float32), axis=0), axis=0)
    lb = lbc - lbc[0:1]
    xp, xs = x_prompt, x_sample
    pk, pv, pf, sk, sv, sf, ph, sh = [], [], [], [], [], [], [], []
    pl, sl, pc, sc = [], [], [], []
    for l in range(DEPTH):
        hp = rmsnorm(xp, norm_mix[l])
        hs = rmsnorm(xs, norm_mix[l])
        if l % 2 == 0:
            e = l // 2
            S0p = jnp.zeros((Bp, HG_HEADS, HG_DK, HG_DV), jnp.float32)
            yp, kp_, vp_, fp_, Sp = even_mixer(hp, w_in_even[e], b_fox_f[e], lb[e], hg_norm[e], w_out_even[e], fox_prompt, S0p)
            fox_s = make_fox_sample(cache_k, cache_v, cache_logf, page_table, e)
            ys, ks_, vs_, fs_, Ss = even_mixer(hs, w_in_even[e], b_fox_f[e], lb[e], hg_norm[e], w_out_even[e], fox_s, state_hgrn[:, e])
            pk.append(kp_); pv.append(vp_); pf.append(fp_); ph.append(Sp)
            sk.append(ks_); sv.append(vs_); sf.append(fs_); sh.append(Ss)
        else:
            o = l // 2
            conv0 = jnp.zeros((Bp, CONV_W - 1, D_LRU), xp.dtype)
            h0 = jnp.zeros((Bp, D_LRU), xp.dtype)
            yp, cp_, lp_ = odd_mixer(hp, w_in_odd[o], conv_w[o], conv_b[o], lru_wa[o], lru_ba[o], lru_wx[o], lru_bx[o], lru_lambda[o], w_out_odd[o], conv0, h0)
            ys, cs_, ls_ = odd_mixer(hs, w_in_odd[o], conv_w[o], conv_b[o], lru_wa[o], lru_ba[o], lru_wx[o], lru_bx[o], lru_lambda[o], w_out_odd[o], state_conv[:, o], state_lru[:, o])
            pc.append(cp_); pl.append(lp_); sc.append(cs_); sl.append(ls_)
        xp = xp + yp
        xs = xs + ys
        xp = xp + sq_relu_mlp(rmsnorm(xp, norm_mlp[l]), mlp_w1[l], mlp_w2[l])
        xs = xs + sq_relu_mlp(rmsnorm(xs, norm_mlp[l]), mlp_w1[l], mlp_w2[l])
    y_prompt = rmsnorm(xp, norm_final)
    y_sample = rmsnorm(xs, norm_final)
    prompt_k = jnp.stack(pk, axis=1)
    prompt_v = jnp.stack(pv, axis=1)
    prompt_logf = jnp.stack(pf, axis=1)
    sample_k = jnp.stack(sk, axis=1)
    sample_v = jnp.stack(sv, axis=1)
    sample_logf = jnp.stack(sf, axis=1)
    prompt_hgrn = jnp.stack(ph, axis=1)
    sample_hgrn = jnp.stack(sh, axis=1)
    prompt_lru = jnp.stack(pl, axis=1)
    sample_lru = jnp.stack(sl, axis=1)
    prompt_conv = jnp.stack(pc, axis=1)
    sample_conv = jnp.stack(sc, axis=1)
    return (y_prompt, y_sample, prompt_k, prompt_v, prompt_logf, sample_k, sample_v, sample_logf,
            prompt_hgrn, sample_hgrn, prompt_lru, sample_lru, prompt_conv, sample_conv)
```

```python
import functools

import jax
import jax.numpy as jnp
from jax import lax
from jax.experimental import pallas as pl
from jax.experimental.pallas import tpu as pltpu

F32 = jnp.float32
BF16 = jnp.bfloat16
NEG_INF = float("-inf")

RMS_EPS = 1e-6
HEAD_DIM = 128
N_HEADS = 8
HEAD_W = N_HEADS * HEAD_DIM
PAGE_SIZE = 128
LRU_BLOCKS = 16
LRU_C = 8.0
HG_CHUNK = 32
LANES = 128
SUBLANES = 8
VMEM_LIMIT_BYTES = 56 * 2**20


def _cparams(*sem):
    return pltpu.CompilerParams(dimension_semantics=sem, vmem_limit_bytes=VMEM_LIMIT_BYTES)


def _rmsnorm(x, g):
    ms = jnp.mean(x * x, axis=-1, keepdims=True)
    return x * lax.rsqrt(ms + RMS_EPS) * g


def _log_sigmoid(x):
    return jnp.minimum(x, 0.0) - jnp.log1p(jnp.exp(-jnp.abs(x)))


def _sigmoid(x):
    return 1.0 / (1.0 + jnp.exp(-x))


def _split3(x):
    hi = x.astype(BF16)
    r1 = x - hi.astype(F32)
    mid = r1.astype(BF16)
    lo = (r1 - mid.astype(F32)).astype(BF16)
    return hi, mid, lo


def _dot01_rhs(x, m01):
    return sum(jnp.dot(p, m01, preferred_element_type=F32) for p in _split3(x))


def _dot01_lhs(m01, x):
    return sum(jnp.dot(m01, p, preferred_element_type=F32) for p in _split3(x))


def _norm_matmul_kernel(*refs, has_extra):
    if has_extra:
        x_ref, g_ref, w_ref, wx_ref, o_ref, ox_ref, h_ref = refs
    else:
        x_ref, g_ref, w_ref, o_ref, h_ref = refs

    @pl.when(pl.program_id(1) == 0)
    def _():
        h = _rmsnorm(x_ref[...], g_ref[...]).astype(BF16)
        h_ref[...] = h
        if has_extra:
            ox_ref[...] = jnp.dot(h, wx_ref[...], preferred_element_type=F32)

    o_ref[...] = jnp.dot(h_ref[...], w_ref[...], preferred_element_type=F32)


def norm_matmul(x, g, w, w_extra=None, *, tm, tn):
    M, K = x.shape
    N = w.shape[1]
    assert M % tm == 0 and N % tn == 0
    has_extra = w_extra is not None
    in_specs = [pl.BlockSpec((tm, K), lambda i, j: (i, 0)),
                pl.BlockSpec((1, K), lambda i, j: (0, 0)),
                pl.BlockSpec((K, tn), lambda i, j: (0, j))]
    out_specs = [pl.BlockSpec((tm, tn), lambda i, j: (i, j))]
    out_shape = [jax.ShapeDtypeStruct((M, N), F32)]
    args = [x, g.reshape(1, K), w]
    if has_extra:
        NX = w_extra.shape[1]
        in_specs.append(pl.BlockSpec((K, NX), lambda i, j: (0, 0)))
        out_specs.append(pl.BlockSpec((tm, NX), lambda i, j: (i, 0)))
        out_shape.append(jax.ShapeDtypeStruct((M, NX), F32))
        args.append(w_extra)
    outs = pl.pallas_call(
        functools.partial(_norm_matmul_kernel, has_extra=has_extra),
        grid=(M // tm, N // tn),
        in_specs=in_specs, out_specs=out_specs, out_shape=out_shape,
        scratch_shapes=[pltpu.VMEM((tm, K), BF16)],
        compiler_params=_cparams("parallel", "arbitrary"),
        name="norm_matmul",
    )(*args)
    return outs if has_extra else outs[0]


def _proj_res_kernel(*refs, n_in):
    a_refs, w_refs = refs[:n_in], refs[n_in:2 * n_in]
    r_ref, o_ref = refs[2 * n_in], refs[2 * n_in + 1]
    acc = r_ref[...]
    for a_ref, w_ref in zip(a_refs, w_refs):
        acc = acc + jnp.dot(a_ref[...], w_ref[...], preferred_element_type=F32)
    o_ref[...] = acc


def proj_residual(a_list, w_list, resid, *, tm, tn):
    M, N = resid.shape
    n_in = len(a_list)
    assert M % tm == 0 and N % tn == 0
    in_specs = ([pl.BlockSpec((tm, a.shape[1]), lambda i, j: (i, 0)) for a in a_list]
                + [pl.BlockSpec((w.shape[0], tn), lambda i, j: (0, j)) for w in w_list]
                + [pl.BlockSpec((tm, tn), lambda i, j: (i, j))])
    return pl.pallas_call(
        functools.partial(_proj_res_kernel, n_in=n_in),
        grid=(M // tm, N // tn),
        in_specs=in_specs,
        out_specs=pl.BlockSpec((tm, tn), lambda i, j: (i, j)),
        out_shape=jax.ShapeDtypeStruct((M, N), F32),
        compiler_params=_cparams("parallel", "arbitrary"),
        name="proj_residual",
    )(*a_list, *w_list, resid)


def _mlp_kernel(*refs, final_norm):
    if final_norm:
        x_ref, g_ref, w1_ref, w2_ref, gf_ref, o_ref, y_ref, h_ref = refs
    else:
        x_ref, g_ref, w1_ref, w2_ref, o_ref, h_ref = refs
    j = pl.program_id(1)

    @pl.when(j == 0)
    def _():
        x = x_ref[...]
        h_ref[...] = _rmsnorm(x, g_ref[...]).astype(BF16)
        o_ref[...] = x

    a = jnp.dot(h_ref[...], w1_ref[...], preferred_element_type=F32)
    a = jnp.maximum(a, 0.0)
    a = (a * a).astype(BF16)
    o_ref[...] += jnp.dot(a, w2_ref[...], preferred_element_type=F32)

    if final_norm:
        @pl.when(j == pl.num_programs(1) - 1)
        def _():
            y_ref[...] = _rmsnorm(o_ref[...], gf_ref[...])


def mlp_residual(x, g, w1, w2, g_final=None, *, tm, tf):
    M, D = x.shape
    FF = w1.shape[1]
    assert M % tm == 0 and FF % tf == 0
    final_norm = g_final is not None
    in_specs = [pl.BlockSpec((tm, D), lambda i, j: (i, 0)),
                pl.BlockSpec((1, D), lambda i, j: (0, 0)),
                pl.BlockSpec((D, tf), lambda i, j: (0, j)),
                pl.BlockSpec((tf, D), lambda i, j: (j, 0))]
    args = [x, g.reshape(1, D), w1, w2]
    out_specs = [pl.BlockSpec((tm, D), lambda i, j: (i, 0))]
    out_shape = [jax.ShapeDtypeStruct((M, D), F32)]
    if final_norm:
        in_specs.append(pl.BlockSpec((1, D), lambda i, j: (0, 0)))
        args.append(g_final.reshape(1, D))
        out_specs.append(pl.BlockSpec((tm, D), lambda i, j: (i, 0)))
        out_shape.append(jax.ShapeDtypeStruct((M, D), F32))
    outs = pl.pallas_call(
        functools.partial(_mlp_kernel, final_norm=final_norm),
        grid=(M // tm, FF // tf),
        in_specs=in_specs, out_specs=out_specs, out_shape=out_shape,
        scratch_shapes=[pltpu.VMEM((tm, D), BF16)],
        compiler_params=_cparams("parallel", "arbitrary"),
        name="mlp_residual",
    )(*args)
    return outs if final_norm else outs[0]


def _fox_pre_kernel(fl_ref, b_ref, logf_ref, cum_ref, *, group):
    logf = _log_sigmoid(fl_ref[...] + b_ref[...])
    logf_ref[...] = logf
    R, L = logf.shape
    upper = (lax.broadcasted_iota(jnp.int32, (L, L), 0) <= lax.broadcasted_iota(jnp.int32, (L, L), 1)).astype(BF16)
    cs = _dot01_rhs(logf, upper)
    tot = jnp.broadcast_to(cs[:, L - 1:L], (R, L))
    rr = lax.broadcasted_iota(jnp.int32, (R, R), 0)
    cc = lax.broadcasted_iota(jnp.int32, (R, R), 1)
    shift = group.bit_length() - 1
    earlier = ((cc < rr) & (jnp.right_shift(cc, shift) == jnp.right_shift(rr, shift))).astype(BF16)
    cum_ref[...] = cs + _dot01_lhs(earlier, tot)


def fox_prompt_logf(fl, b_f):
    B, T, H = fl.shape
    group = T // LANES
    assert T % LANES == 0 and group & (group - 1) == 0
    rows = B * H * group
    fl_rows = jnp.transpose(fl, (0, 2, 1)).reshape(rows, LANES)
    b_rows = jnp.broadcast_to(b_f.reshape(1, H, 1), (B, H, group)).reshape(rows, 1)
    logf_rows, cum_rows = pl.pallas_call(
        functools.partial(_fox_pre_kernel, group=group),
        out_shape=[jax.ShapeDtypeStruct((rows, LANES), F32)] * 2,
        name="fox_prompt_logf",
    )(fl_rows, b_rows)
    logf = jnp.transpose(logf_rows.reshape(B, H, T), (0, 2, 1))
    return logf, cum_rows.reshape(B, H, 1, T)


def _fox_attn_kernel(q_ref, k_ref, v_ref, c_ref, o_ref, *, scale, tq):
    qi = pl.program_id(2)
    q0 = pl.multiple_of(qi * tq, tq)
    q = (q_ref[0] * scale).astype(BF16)
    c0 = c_ref[0, 0, :, pl.ds(q0, LANES)][:, 0:1]

    def scores(k0):
        k = k_ref[0, pl.ds(k0, tq), :].astype(BF16)
        s = lax.dot_general(q, k, (((1,), (1,)), ((), ())), preferred_element_type=F32)
        return s - (c_ref[0, 0, :, pl.ds(k0, tq)] - c0)

    def update(carry, s, k0):
        m, l, acc = carry
        m_new = jnp.maximum(m, jnp.max(s, axis=1, keepdims=True))
        alpha = jnp.exp(m - m_new)
        p = jnp.exp(s - m_new)
        l = alpha * l + jnp.sum(p, axis=1, keepdims=True)
        v = v_ref[0, pl.ds(k0, tq), :].astype(BF16)
        acc = alpha * acc + jnp.dot(p.astype(BF16), v, preferred_element_type=F32)
        return m_new, l, acc

    def off_diag(kt, carry):
        k0 = pl.multiple_of(kt * tq, tq)
        return update(carry, scores(k0), k0)

    init = (jnp.full((tq, 1), NEG_INF, F32), jnp.zeros((tq, 1), F32), jnp.zeros((tq, HEAD_DIM), F32))
    carry = lax.fori_loop(0, qi, off_diag, init)
    s = scores(q0)
    causal = lax.broadcasted_iota(jnp.int32, (tq, tq), 1) <= lax.broadcasted_iota(jnp.int32, (tq, tq), 0)
    _, l, acc = update(carry, jnp.where(causal, s, NEG_INF), q0)
    o_ref[0] = (acc / l).astype(o_ref.dtype)


def fox_prompt_attention(z, cum, *, tq):
    B, T, _ = z.shape
    assert T % tq == 0
    return pl.pallas_call(
        functools.partial(_fox_attn_kernel, scale=HEAD_DIM ** -0.5, tq=tq),
        grid=(B, N_HEADS, T // tq),
        in_specs=[pl.BlockSpec((1, tq, HEAD_DIM), lambda b, h, i: (b, i, h)),
                  pl.BlockSpec((1, T, HEAD_DIM), lambda b, h, i: (b, 0, N_HEADS + h)),
                  pl.BlockSpec((1, T, HEAD_DIM), lambda b, h, i: (b, 0, 2 * N_HEADS + h)),
                  pl.BlockSpec((1, 1, 1, T), lambda b, h, i: (b, h, 0, 0))],
        out_specs=pl.BlockSpec((1, tq, HEAD_DIM), lambda b, h, i: (b, i, h)),
        out_shape=jax.ShapeDtypeStruct((B, T, HEAD_W), BF16),
        compiler_params=_cparams("parallel", "parallel", "arbitrary"),
        name="fox_prompt_attention",
    )(z, z, z, cum)


def _decode_bias_kernel(pt_ref, *refs, pps):
    del pt_ref
    page_refs, fl_ref = refs[:pps], refs[pps]
    lnew_ref, bias_ref, x_ref = refs[pps + 1], refs[pps + 2], refs[pps + 3]
    g = pl.program_id(1)
    for i, r in enumerate(page_refs):
        x_ref[pl.ds(g * pps + i, 1), :] = r[0, 0]

    @pl.when(g == pl.num_programs(1) - 1)
    def _():
        x = x_ref[...]
        n_pages, L = x.shape
        lnew = _log_sigmoid(fl_ref[0])
        lnew_ref[0] = lnew
        li = lax.broadcasted_iota(jnp.int32, (L, L), 0)
        lj = lax.broadcasted_iota(jnp.int32, (L, L), 1)
        same_head = jnp.bitwise_and(li, N_HEADS - 1) == jnp.bitwise_and(lj, N_HEADS - 1)
        later = (same_head & (li > lj)).astype(BF16)
        within = _dot01_rhs(x, later)
        page_tot = _dot01_rhs(x, same_head.astype(BF16))
        pi = lax.broadcasted_iota(jnp.int32, (n_pages, n_pages), 0)
        pj = lax.broadcasted_iota(jnp.int32, (n_pages, n_pages), 1)
        later_pages = (pj > pi).astype(BF16)
        bias_ref[0] = within + _dot01_lhs(later_pages, page_tot) + lnew


def fox_decode_bias(logf_pages, page_table, fl_tiled, layer, *, pps):
    DB, n_pages = page_table.shape
    L = logf_pages.shape[-1]
    assert n_pages % pps == 0

    def page_spec(i):
        return pl.BlockSpec((1, 1, 1, L), lambda b, g, pt: (pt[b, g * pps + i], layer, 0, 0))

    return pl.pallas_call(
        functools.partial(_decode_bias_kernel, pps=pps),
        grid_spec=pltpu.PrefetchScalarGridSpec(
            num_scalar_prefetch=1,
            grid=(DB, n_pages // pps),
            in_specs=[page_spec(i) for i in range(pps)] + [pl.BlockSpec((1, 1, L), lambda b, g, pt: (b, 0, 0))],
            out_specs=[pl.BlockSpec((1, 1, L), lambda b, g, pt: (b, 0, 0)),
                       pl.BlockSpec((1, n_pages, L), lambda b, g, pt: (b, 0, 0))],
            scratch_shapes=[pltpu.VMEM((n_pages, L), F32)]),
        out_shape=[jax.ShapeDtypeStruct((DB, 1, L), F32), jax.ShapeDtypeStruct((DB, n_pages, L), F32)],
        compiler_params=_cparams("parallel", "arbitrary"),
        name="fox_decode_bias",
    )(page_table, *([logf_pages] * pps), fl_tiled)


def _decode_attn_kernel(pt_ref, *refs, pps, scale):
    del pt_ref
    k_refs, v_refs = refs[:pps], refs[pps:2 * pps]
    q_ref, kn_ref, vn_ref, bias_ref, o_ref, m_ref, l_ref, acc_ref = refs[2 * pps:]
    g = pl.program_id(1)

    @pl.when(g == 0)
    def _():
        m_ref[...] = jnp.full(m_ref.shape, NEG_INF, F32)
        l_ref[...] = jnp.zeros(l_ref.shape, F32)
        acc_ref[...] = jnp.zeros(acc_ref.shape, F32)

    q = q_ref[0] * scale
    qb = q.astype(BF16)
    L = PAGE_SIZE * N_HEADS
    own_head = (jnp.bitwise_and(lax.broadcasted_iota(jnp.int32, (N_HEADS, L), 1), N_HEADS - 1)
                == lax.broadcasted_iota(jnp.int32, (N_HEADS, L), 0))
    for i in range(pps):
        k = k_refs[i][0, 0].astype(BF16)
        s = lax.dot_general(qb, k, (((1,), (1,)), ((), ())), preferred_element_type=F32)
        s = jnp.where(own_head, s + bias_ref[0, 0, pl.ds(i, 1), :], NEG_INF)
        m = m_ref[...]
        m_new = jnp.maximum(m, jnp.max(s, axis=1, keepdims=True))
        alpha = jnp.exp(m - m_new)
        p = jnp.exp(s - m_new)
        l_ref[...] = alpha * l_ref[...] + jnp.sum(p, axis=1, keepdims=True)
        v = v_refs[i][0, 0].astype(BF16)
        acc_ref[...] = alpha * acc_ref[...] + jnp.dot(p.astype(BF16), v, preferred_element_type=F32)
        m_ref[...] = m_new

    @pl.when(g == pl.num_programs(1) - 1)
    def _():
        s_new = jnp.sum(q * kn_ref[0], axis=1, keepdims=True)
        m = m_ref[...]
        m_f = jnp.maximum(m, s_new)
        a = jnp.exp(m - m_f)
        pn = jnp.exp(s_new - m_f)
        o_ref[0] = (acc_ref[...] * a + pn * vn_ref[0]) / (l_ref[...] * a + pn)


def fox_decode_attention(k_pages, v_pages, page_table, q, k_new, v_new, bias, layer, *, pps):
    DB, n_pages = page_table.shape
    L = PAGE_SIZE * N_HEADS
    assert n_pages % pps == 0
    n_g = n_pages // pps
    bias4 = bias.reshape(DB, n_g, pps, L)

    def page_spec(i):
        return pl.BlockSpec((1, 1, L, HEAD_DIM), lambda b, g, pt: (pt[b, g * pps + i], layer, 0, 0))

    tok_spec = pl.BlockSpec((1, N_HEADS, HEAD_DIM), lambda b, g, pt: (b, 0, 0))
    return pl.pallas_call(
        functools.partial(_decode_attn_kernel, pps=pps, scale=HEAD_DIM ** -0.5),
        grid_spec=pltpu.PrefetchScalarGridSpec(
            num_scalar_prefetch=1,
            grid=(DB, n_g),
            in_specs=([page_spec(i) for i in range(pps)] * 2
                      + [tok_spec, tok_spec, tok_spec,
                         pl.BlockSpec((1, 1, pps, L), lambda b, g, pt: (b, g, 0, 0))]),
            out_specs=tok_spec,
            scratch_shapes=[pltpu.VMEM((N_HEADS, 1), F32), pltpu.VMEM((N_HEADS, 1), F32),
                            pltpu.VMEM((N_HEADS, HEAD_DIM), F32)]),
        out_shape=jax.ShapeDtypeStruct((DB, N_HEADS, HEAD_DIM), F32),
        compiler_params=_cparams("parallel", "arbitrary"),
        name="fox_decode_attention",
    )(page_table, *([k_pages] * pps), *([v_pages] * pps), q, k_new, v_new, bias4)


def _hgrn_gates(hf, llb, l1mlb):
    b = l1mlb + _log_sigmoid(hf)
    return jnp.maximum(llb, b) + jnp.log1p(jnp.exp(-jnp.abs(llb - b)))


def _hgrn_prompt_kernel(hq_ref, hf_ref, hi_ref, hg_ref, llb_ref, l1m_ref, gn_ref, o_ref, s_ref,
                        st_ref, q_s, k_s, bc_s, *, tt):
    C = HG_CHUNK
    D = HEAD_DIM

    @pl.when(pl.program_id(1) == 0)
    def _():
        st_ref[...] = jnp.zeros(st_ref.shape, F32)

    lower = (lax.broadcasted_iota(jnp.int32, (C, C), 0) >= lax.broadcasted_iota(jnp.int32, (C, C), 1)).astype(BF16)
    starts = range(0, C, SUBLANES)
    lane_id = {t0: lax.broadcasted_iota(jnp.int32, (C - t0, C), 1) for t0 in starts}
    row_id = {t0: lax.broadcasted_iota(jnp.int32, (C - t0, D), 0) + t0 for t0 in starts}

    def chunk(c, carry):
        r0 = pl.multiple_of(c * C, C)
        logf = _hgrn_gates(hf_ref[0, pl.ds(r0, C), :], llb_ref[...], l1m_ref[...])
        k_s[...] = 1.0 - jnp.exp(logf)
        bc_s[...] = _dot01_lhs(lower, logf)
        hq = hq_ref[0, pl.ds(r0, C), :]
        q_s[...] = hq * _sigmoid(hq)

        def head(h, carry):
            c0 = pl.multiple_of(h * D, D)
            q = q_s[:, pl.ds(c0, D)]
            kk = k_s[:, pl.ds(c0, D)]
            bc = bc_s[:, pl.ds(c0, D)]
            v = hi_ref[0, pl.ds(r0, C), pl.ds(c0, D)].astype(BF16)
            st = st_ref[h]
            blast = bc[C - 1:C, :]
            a = jnp.zeros((C, C), F32)
            for s in range(C):
                t0 = (s // SUBLANES) * SUBLANES
                d = bc[t0:, :] - bc[s:s + 1, :]
                d = jnp.where(row_id[t0] >= s, d, NEG_INF)
                col = jnp.sum(q[t0:, :] * kk[s:s + 1, :] * jnp.exp(d), axis=1, keepdims=True)
                upd = jnp.where(lane_id[t0] == s, col, a[t0:, :])
                a = upd if t0 == 0 else jnp.concatenate([a[:t0, :], upd], axis=0)
            qd = (q * jnp.exp(bc)).astype(BF16)
            o = lax.dot_general(qd, st.astype(BF16), (((1,), (1,)), ((), ())), preferred_element_type=F32)
            o = o + jnp.dot(a.astype(BF16), v, preferred_element_type=F32)
            kd = (kk * jnp.exp(blast - bc)).astype(BF16)
            st_ref[h] = st * jnp.exp(blast) + lax.dot_general(
                v, kd, (((0,), (0,)), ((), ())), preferred_element_type=F32)
            on = _rmsnorm(o, gn_ref[:, pl.ds(c0, D)])
            hg = hg_ref[0, pl.ds(r0, C), pl.ds(c0, D)]
            o_ref[0, pl.ds(r0, C), pl.ds(c0, D)] = (on * (hg * _sigmoid(hg))).astype(o_ref.dtype)
            return carry

        return lax.fori_loop(0, N_HEADS, head, carry, unroll=2)

    lax.fori_loop(0, tt // C, chunk, 0)

    @pl.when(pl.program_id(1) == pl.num_programs(1) - 1)
    def _():
        for h in range(N_HEADS):
            s_ref[0, h] = st_ref[h].T


def hgrn_prompt(z, llb, l1mlb, g_norm, *, tt):
    B, T, _ = z.shape
    assert T % tt == 0 and tt % HG_CHUNK == 0
    col = lambda cb: pl.BlockSpec((1, tt, HEAD_W), lambda b, t: (b, t, cb))
    par = pl.BlockSpec((1, HEAD_W), lambda b, t: (0, 0))
    return pl.pallas_call(
        functools.partial(_hgrn_prompt_kernel, tt=tt),
        grid=(B, T // tt),
        in_specs=[col(3), col(4), col(5), col(6), par, par, par],
        out_specs=[pl.BlockSpec((1, tt, HEAD_W), lambda b, t: (b, t, 0)),
                   pl.BlockSpec((1, N_HEADS, HEAD_DIM, HEAD_DIM), lambda b, t: (b, 0, 0, 0))],
        out_shape=[jax.ShapeDtypeStruct((B, T, HEAD_W), BF16),
                   jax.ShapeDtypeStruct((B, N_HEADS, HEAD_DIM, HEAD_DIM), F32)],
        scratch_shapes=[pltpu.VMEM((N_HEADS, HEAD_DIM, HEAD_DIM), F32),
                        pltpu.VMEM((HG_CHUNK, HEAD_W), F32), pltpu.VMEM((HG_CHUNK, HEAD_W), F32),
                        pltpu.VMEM((HG_CHUNK, HEAD_W), F32)],
        compiler_params=_cparams("parallel", "arbitrary"),
        name="hgrn_prompt",
    )(z, z, z, z, llb.reshape(1, HEAD_W), l1mlb.reshape(1, HEAD_W), g_norm.reshape(1, HEAD_W))


def _row_to_col(row):
    n = row.shape[1]
    eye = lax.broadcasted_iota(jnp.int32, (n, n), 0) == lax.broadcasted_iota(jnp.int32, (n, n), 1)
    return jnp.sum(jnp.where(eye, row, 0.0), axis=1, keepdims=True)


def _hgrn_step_kernel(hq_ref, hf_ref, hi_ref, hg_ref, llb_ref, l1m_ref, gn_ref, s0_ref, o_ref, s_ref):
    logf = _hgrn_gates(hf_ref[0, 0], llb_ref[0], l1m_ref[0])
    f = jnp.exp(logf)
    hq = hq_ref[0, 0]
    q = hq * _sigmoid(hq)
    s_new = _row_to_col(f) * s0_ref[0, 0, 0] + _row_to_col(1.0 - f) * hi_ref[0, 0]
    s_ref[0, 0] = s_new
    o = jnp.sum(_row_to_col(q) * s_new, axis=0, keepdims=True)
    hg = hg_ref[0, 0]
    o_ref[0, 0] = _rmsnorm(o, gn_ref[0]) * (hg * _sigmoid(hg))


def hgrn_step(z4, llb, l1mlb, g_norm, state, layer):
    DB = z4.shape[0]
    D = HEAD_DIM
    col = lambda cb: pl.BlockSpec((1, 1, 1, D), lambda b, h: (b, cb * N_HEADS + h, 0, 0))
    par = pl.BlockSpec((1, 1, D), lambda b, h: (h, 0, 0))
    return pl.pallas_call(
        _hgrn_step_kernel,
        grid=(DB, N_HEADS),
        in_specs=[col(3), col(4), col(5), col(6), par, par, par,
                  pl.BlockSpec((1, 1, 1, D, D), lambda b, h: (b, layer, h, 0, 0))],
        out_specs=[pl.BlockSpec((1, 1, 1, D), lambda b, h: (b, h, 0, 0)),
                   pl.BlockSpec((1, 1, D, D), lambda b, h: (b, h, 0, 0))],
        out_shape=[jax.ShapeDtypeStruct((DB, N_HEADS, 1, D), F32),
                   jax.ShapeDtypeStruct((DB, N_HEADS, D, D), F32)],
        compiler_params=_cparams("parallel", "parallel"),
        name="hgrn_step",
    )(z4, z4, z4, z4, llb.reshape(N_HEADS, 1, D), l1mlb.reshape(N_HEADS, 1, D),
      g_norm.reshape(N_HEADS, 1, D), state)


def _gelu_tanh(x):
    return 0.5 * x * (1.0 + jnp.tanh(0.7978845608028654 * (x + 0.044715 * (x * x * x))))


def _lru_gates(c, wa_ref, wx_ref, ba, bx, lam):
    bw = c.shape[1] // LRU_BLOCKS
    cb = c.astype(BF16)
    ra, xa = [], []
    for n in range(LRU_BLOCKS):
        blk = cb[:, n * bw:(n + 1) * bw]
        ra.append(jnp.dot(blk, wa_ref[n], preferred_element_type=F32))
        xa.append(jnp.dot(blk, wx_ref[n], preferred_element_type=F32))
    r = _sigmoid(jnp.concatenate(ra, axis=1) + ba)
    ig = _sigmoid(jnp.concatenate(xa, axis=1) + bx)
    softplus_neg_lam = jnp.maximum(-lam, 0.0) + jnp.log1p(jnp.exp(-jnp.abs(lam)))
    log_a = -LRU_C * r * softplus_neg_lam
    a = jnp.exp(log_a)
    b = jnp.sqrt(1.0 - jnp.exp(2.0 * log_a)) * (ig * c)
    return a, b


def _lru_prompt_kernel(gate_ref, u_ref, cw_ref, cb_ref, wa_ref, wx_ref, ba_ref, bx_ref, lam_ref,
                       y_ref, conv_ref, hlast_ref, tail_ref, h_ref, a_s, b_s, hs_s, *, tt, conv_w):
    t = pl.program_id(1)
    G = SUBLANES

    @pl.when(t == 0)
    def _():
        tail_ref[...] = jnp.zeros(tail_ref.shape, F32)
        h_ref[...] = jnp.zeros(h_ref.shape, F32)

    u = u_ref[0]
    xp = jnp.concatenate([tail_ref[...], u], axis=0)
    c = cb_ref[...]
    for j in range(conv_w):
        off = G - (conv_w - 1) + j
        c = c + xp[off:off + tt, :] * cw_ref[pl.ds(j, 1), :]
    tail_ref[...] = u[tt - G:, :]
    a, b = _lru_gates(c, wa_ref, wx_ref, ba_ref[...], bx_ref[...], lam_ref[...])
    a_s[...] = a
    b_s[...] = b

    row = lax.broadcasted_iota(jnp.int32, (G, a.shape[1]), 0)

    def group(g, h_in):
        r0 = pl.multiple_of(g * G, G)
        ag = a_s[pl.ds(r0, G), :]
        bg = b_s[pl.ds(r0, G), :]
        sh = 1
        while sh < G:
            keep = row >= sh
            bg = jnp.where(keep, ag * pltpu.roll(bg, sh, 0) + bg, bg)
            ag = jnp.where(keep, ag * pltpu.roll(ag, sh, 0), ag)
            sh *= 2
        hs = ag * h_in + bg
        hs_s[pl.ds(r0, G), :] = hs
        return hs[G - 1:G, :]

    h_out = lax.fori_loop(0, tt // G, group, h_ref[...])
    h_ref[...] = h_out
    y_ref[0] = (_gelu_tanh(gate_ref[0]) * hs_s[...]).astype(y_ref.dtype)

    @pl.when(t == pl.num_programs(1) - 1)
    def _():
        conv_ref[0] = u[tt - (conv_w - 1):, :]
        hlast_ref[0] = h_out


def lru_prompt(z, conv_w, conv_b, wa, wx, ba, bx, lam, *, tt):
    B, T, D2 = z.shape
    D = D2 // 2
    CW = conv_w.shape[0]
    bw = D // LRU_BLOCKS
    assert T % tt == 0 and tt % SUBLANES == 0
    par = pl.BlockSpec((1, D), lambda b, t: (0, 0))
    wspec = pl.BlockSpec((LRU_BLOCKS, bw, bw), lambda b, t: (0, 0, 0))
    return pl.pallas_call(
        functools.partial(_lru_prompt_kernel, tt=tt, conv_w=CW),
        grid=(B, T // tt),
        in_specs=[pl.BlockSpec((1, tt, D), lambda b, t: (b, t, 0)),
                  pl.BlockSpec((1, tt, D), lambda b, t: (b, t, 1)),
                  pl.BlockSpec((CW, D), lambda b, t: (0, 0)), par, wspec, wspec, par, par, par],
        out_specs=[pl.BlockSpec((1, tt, D), lambda b, t: (b, t, 0)),
                   pl.BlockSpec((1, CW - 1, D), lambda b, t: (b, 0, 0)),
                   pl.BlockSpec((1, 1, D), lambda b, t: (b, 0, 0))],
        out_shape=[jax.ShapeDtypeStruct((B, T, D), BF16),
                   jax.ShapeDtypeStruct((B, CW - 1, D), F32),
                   jax.ShapeDtypeStruct((B, 1, D), F32)],
        scratch_shapes=[pltpu.VMEM((SUBLANES, D), F32), pltpu.VMEM((1, D), F32),
                        pltpu.VMEM((tt, D), F32), pltpu.VMEM((tt, D), F32), pltpu.VMEM((tt, D), F32)],
        compiler_params=_cparams("parallel", "arbitrary"),
        name="lru_prompt",
    )(z, z, conv_w, conv_b.reshape(1, D), wa, wx, ba.reshape(1, D), bx.reshape(1, D), lam.reshape(1, D))


def _lru_step_kernel(gate_ref, u_ref, cs_ref, h0_ref, cw_ref, cb_ref, wa_ref, wx_ref, ba_ref, bx_ref,
                     lam_ref, y_ref, conv_ref, h_ref, *, conv_w):
    u = u_ref[...]
    c = cb_ref[...] + u * cw_ref[pl.ds(conv_w - 1, 1), :]
    for j in range(conv_w - 1):
        c = c + cs_ref[j] * cw_ref[pl.ds(j, 1), :]
    a, b = _lru_gates(c, wa_ref, wx_ref, ba_ref[...], bx_ref[...], lam_ref[...])
    h = a * h0_ref[...] + b
    h_ref[...] = h
    y_ref[...] = (_gelu_tanh(gate_ref[...]) * h).astype(y_ref.dtype)
    for j in range(conv_w - 2):
        conv_ref[j] = cs_ref[j + 1]
    conv_ref[conv_w - 2] = u


def lru_step(z, conv_state_t, h0, conv_w, conv_b, wa, wx, ba, bx, lam):
    DB, D2 = z.shape
    D = D2 // 2
    CW = conv_w.shape[0]
    bw = D // LRU_BLOCKS
    full = lambda *shape: pl.BlockSpec(shape, lambda i: (0,) * len(shape))
    return pl.pallas_call(
        functools.partial(_lru_step_kernel, conv_w=CW),
        grid=(1,),
        in_specs=[pl.BlockSpec((DB, D), lambda i: (0, 0)), pl.BlockSpec((DB, D), lambda i: (0, 1)),
                  full(CW - 1, DB, D), full(DB, D), full(CW, D), full(1, D),
                  full(LRU_BLOCKS, bw, bw), full(LRU_BLOCKS, bw, bw), full(1, D), full(1, D), full(1, D)],
        out_specs=[full(DB, D), full(CW - 1, DB, D), full(DB, D)],
        out_shape=[jax.ShapeDtypeStruct((DB, D), BF16),
                   jax.ShapeDtypeStruct((CW - 1, DB, D), F32),
                   jax.ShapeDtypeStruct((DB, D), F32)],
        compiler_params=_cparams("arbitrary"),
        name="lru_step",
    )(z, z, conv_state_t, h0, conv_w, conv_b.reshape(1, D), wa, wx, ba.reshape(1, D), bx.reshape(1, D),
      lam.reshape(1, D))


def _tiles(m):
    return 512 if m % 512 == 0 else m


def kernel(x_prompt, x_sample, cache_k, cache_v, cache_logf, state_hgrn, state_lru, state_conv, page_table,
           norm_mix, norm_mlp, norm_final, w_in_even, b_fox_f, hg_lb_logits, hg_norm, w_out_even, w_in_odd,
           conv_w, conv_b, lru_wa, lru_ba, lru_wx, lru_bx, lru_lambda, w_out_odd, mlp_w1, mlp_w2):
    B, T, D = x_prompt.shape
    DB = x_sample.shape[0]
    depth = norm_mix.shape[0]
    n_phys, n_even = cache_k.shape[0], cache_k.shape[1]
    H, Dh, W = N_HEADS, HEAD_DIM, HEAD_W
    L = PAGE_SIZE * H

    lbc = jnp.cumsum(jax.nn.softmax(hg_lb_logits.astype(F32), axis=0), axis=0)
    lb = lbc - lbc[0:1]
    log_lb, log1m_lb = jnp.log(lb), jnp.log1p(-lb)

    k_pages = cache_k.reshape(n_phys, n_even, L, Dh)
    v_pages = cache_v.reshape(n_phys, n_even, L, Dh)
    logf_pages = cache_logf.reshape(n_phys, n_even, 1, L)

    xp = x_prompt.reshape(B * T, D)
    xs = x_sample.reshape(DB, D)
    tmp, tms = _tiles(B * T), _tiles(DB)

    pk, pv, pf, sk, sv, sf, ph, sh, plru, slru, pconv, sconv = ([] for _ in range(12))
    for l in range(depth):
        last = l == depth - 1
        if l % 2 == 0:
            e = l // 2
            w = w_in_even[e]
            w_main = jnp.concatenate([w[:, :3 * W], w[:, 3 * W + H:]], axis=1).astype(BF16)
            w_fl = jnp.pad(w[:, 3 * W:3 * W + H], ((0, 0), (0, LANES - H))).astype(BF16)
            wo = w_out_even[e].astype(BF16)
            wo_fox, wo_hg = wo[:W], wo[W:]

            z, flp = norm_matmul(xp, norm_mix[l], w_main, w_fl, tm=tmp, tn=1024)
            z3 = z.reshape(B, T, 7 * W)
            logf, cum = fox_prompt_logf(flp.reshape(B, T, LANES)[:, :, :H], b_fox_f[e])
            o_fox = fox_prompt_attention(z3, cum, tq=512)
            o_hg, s_p = hgrn_prompt(z3, log_lb[e], log1m_lb[e], hg_norm[e], tt=256)
            xp = proj_residual([o_fox.reshape(B * T, W), o_hg.reshape(B * T, W)], [wo_fox, wo_hg], xp,
                               tm=tmp, tn=1024)
            pk.append(z3[:, :, W:2 * W].reshape(B, T, H, Dh))
            pv.append(z3[:, :, 2 * W:3 * W].reshape(B, T, H, Dh))
            pf.append(logf)
            ph.append(s_p)

            zs, fls = norm_matmul(xs, norm_mix[l], w_main, w_fl, tm=tms, tn=1024)
            fl_tiled = jnp.tile(fls[:, :H] + b_fox_f[e], (1, PAGE_SIZE)).reshape(DB, 1, L)
            lnew, bias = fox_decode_bias(logf_pages, page_table, fl_tiled, e, pps=8)
            q_s = zs[:, :W].reshape(DB, H, Dh)
            k_s = zs[:, W:2 * W].reshape(DB, H, Dh)
            v_s = zs[:, 2 * W:3 * W].reshape(DB, H, Dh)
            o_fox_s = fox_decode_attention(k_pages, v_pages, page_table, q_s, k_s, v_s, bias, e, pps=4)
            o_hg_s, s_s = hgrn_step(zs.reshape(DB, 7 * H, 1, Dh), log_lb[e], log1m_lb[e], hg_norm[e],
                                    state_hgrn, e)
            xs = proj_residual([o_fox_s.reshape(DB, W).astype(BF16), o_hg_s.reshape(DB, W).astype(BF16)],
                               [wo_fox, wo_hg], xs, tm=tms, tn=1024)
            sk.append(k_s.reshape(DB, 1, H, Dh))
            sv.append(v_s.reshape(DB, 1, H, Dh))
            sf.append(lnew[:, :, :H])
            sh.append(s_s)
        else:
            o = l // 2
            w_in = w_in_odd[o].astype(BF16)
            wo = w_out_odd[o].astype(BF16)
            wa, wx = lru_wa[o].astype(BF16), lru_wx[o].astype(BF16)
            par = (conv_w[o], conv_b[o], wa, wx, lru_ba[o], lru_bx[o], lru_lambda[o])

            z = norm_matmul(xp, norm_mix[l], w_in, tm=tmp, tn=1024)
            y, conv_p, h_p = lru_prompt(z.reshape(B, T, 2 * D), *par, tt=256)
            xp = proj_residual([y.reshape(B * T, D)], [wo], xp, tm=tmp, tn=1024)
            pconv.append(conv_p)
            plru.append(h_p.reshape(B, D))

            zs = norm_matmul(xs, norm_mix[l], w_in, tm=tms, tn=1024)
            ys, conv_s, h_s = lru_step(zs, jnp.transpose(state_conv[:, o], (1, 0, 2)), state_lru[:, o], *par)
            xs = proj_residual([ys], [wo], xs, tm=tms, tn=1024)
            sconv.append(jnp.transpose(conv_s, (1, 0, 2)))
            slru.append(h_s)

        w1, w2 = mlp_w1[l].astype(BF16), mlp_w2[l].astype(BF16)
        if last:
            xp, y_prompt = mlp_residual(xp, norm_mlp[l], w1, w2, norm_final, tm=tmp, tf=512)
            xs, y_sample = mlp_residual(xs, norm_mlp[l], w1, w2, norm_final, tm=tms, tf=512)
        else:
            xp = mlp_residual(xp, norm_mlp[l], w1, w2, tm=tmp, tf=512)
            xs = mlp_residual(xs, norm_mlp[l], w1, w2, tm=tms, tf=512)

    st = lambda xs_: jnp.stack(xs_, axis=1)
    return (y_prompt.reshape(B, T, D), y_sample.reshape(DB, 1, D),
            st(pk), st(pv), st(pf), st(sk), st(sv), st(sf), st(ph), st(sh),
            st(plru), st(slru), st(pconv), st(sconv))
```
